```python
import jax, jax.numpy as jnp
from jax import lax
import numpy as np

D_MODEL = 1024
BATCH = 16
SEQ = 4096
DEPTH = 4
DEC_BATCH = 16
DEC_SEQ = 32
PAST_LEN = 4096

CHUNK = 64
BRANCH_W = 512
N_BRANCH = 3
CONV_CH = BRANCH_W
CONV_WIDTH = 31
SGU_CH = BRANCH_W
SGU_GROUPS = 4
SGU_GROUP_CH = SGU_CH // SGU_GROUPS
SGU_CHUNK = 128
SB_HEADS = 8
SB_HEAD_DIM = BRANCH_W // SB_HEADS
SB_WIDTH = SB_HEADS * SB_HEAD_DIM
SB_QBLOCK = 128
D_FF = 2816
FFN_CONV_WIDTH = 3
DN_ALPHA = (2 * DEPTH) ** 0.25
DN_BETA = (8 * DEPTH) ** -0.25
LN_EPS = 1e-5
OFF_A = 2 * CONV_CH
OFF_B = OFF_A + 2 * SGU_CH
OFF_C = OFF_B + 3 * SB_WIDTH
D_IN = OFF_C + N_BRANCH * D_MODEL
IN_SPLITS = (CONV_CH, OFF_A, OFF_A + SGU_CH, OFF_B, OFF_B + SB_WIDTH, OFF_B + 2 * SB_WIDTH, OFF_C)

kernel_name = 'hybrid_stream_encoder_step'


def layer_norm(x, g, b):
    xf = x.astype(jnp.float32)
    mu = jnp.mean(xf, axis=-1, keepdims=True)
    var = jnp.mean(jnp.square(xf - mu), axis=-1, keepdims=True)
    return ((xf - mu) * lax.rsqrt(var + LN_EPS) * g + b).astype(x.dtype)


def causal_dwconv(x_ext, w, b):
    ch = x_ext.shape[-1]
    y = lax.conv_general_dilated(x_ext, w[:, None, :].astype(x_ext.dtype), window_strides=(1,),
                                 padding='VALID', dimension_numbers=('NWC', 'WIO', 'NWC'),
                                 feature_group_count=ch)
    return y + b


def stick_breaking(q, k, v, q_pos, k_pos):
    z = jnp.einsum('bqhd,bkhd->bhqk', q, k, preferred_element_type=jnp.float32) * (SB_HEAD_DIM ** -0.5)
    mask = k_pos[None, :] < q_pos[:, None]
    log_keep = jnp.where(mask, jax.nn.log_sigmoid(-z), 0.0)
    later = lax.cumsum(log_keep, axis=3, reverse=True) - log_keep
    attn = jnp.where(mask, jnp.exp(jax.nn.log_sigmoid(z) + later), 0.0)
    return jnp.einsum('bhqk,bkhd->bqhd', attn.astype(v.dtype), v)


def stick_breaking_prompt(q, k, v):
    seq = q.shape[1]
    pos = jnp.arange(seq, dtype=jnp.int32)
    outs = []
    for i in range(seq // SB_QBLOCK):
        lo, hi = i * SB_QBLOCK, (i + 1) * SB_QBLOCK
        outs.append(stick_breaking(q[:, lo:hi], k[:, :hi], v[:, :hi], pos[lo:hi], pos[:hi]))
    return jnp.concatenate(outs, axis=1)


def spatial_gate(v, w_s, b_s):
    n_rows = v.shape[2]
    w = jnp.tril(w_s[:, :n_rows, :n_rows]).astype(v.dtype)
    return jnp.einsum('gts,bnsgc->bntgc', w, v) + b_s[:, :n_rows].T[None, None, :, :, None]


def trunk_layer(x, conv_hist, ffn_hist, k_hist, v_hist,
                w_in, b_gate, conv_w, conv_b, conv_ln_g, conv_ln_b,
                sgu_ln_g, sgu_ln_b, sgu_w, sgu_b, w_branch, w_out, ln1_g, ln1_b,
                w_up, ffn_conv_w, ffn_conv_b, w_down, ln2_g, ln2_b):
    bsz, t_len, _ = x.shape
    prompt = k_hist is None
    a_in, a_gate, u, v, q, k, vv, g = jnp.split(x @ w_in, IN_SPLITS, axis=-1)

    glu = a_in * jax.nn.sigmoid(a_gate)
    if prompt:
        conv_hist = jnp.zeros((bsz, CONV_WIDTH - 1, CONV_CH), glu.dtype)
    conv_ext = jnp.concatenate([conv_hist.astype(glu.dtype), glu], axis=1)
    y_a = jax.nn.silu(layer_norm(causal_dwconv(conv_ext, conv_w, conv_b), conv_ln_g, conv_ln_b))

    v_n = layer_norm(jax.nn.gelu(v, approximate=False), sgu_ln_g, sgu_ln_b)
    rows = SGU_CHUNK if prompt else t_len
    mixed = spatial_gate(v_n.reshape(bsz, t_len // rows, rows, SGU_GROUPS, SGU_GROUP_CH), sgu_w, sgu_b)
    y_b = jax.nn.gelu(u, approximate=False) * mixed.reshape(bsz, t_len, SGU_CH)

    q = q.reshape(bsz, t_len, SB_HEADS, SB_HEAD_DIM)
    k = k.reshape(bsz, t_len, SB_HEADS, SB_HEAD_DIM)
    vv = vv.reshape(bsz, t_len, SB_HEADS, SB_HEAD_DIM)
    if prompt:
        o = stick_breaking_prompt(q, k, vv)
    else:
        past = k_hist.shape[1]
        k_all = jnp.concatenate([k_hist.astype(k.dtype), k], axis=1)
        v_all = jnp.concatenate([v_hist.astype(vv.dtype), vv], axis=1)
        o = stick_breaking(q, k_all, v_all, past + jnp.arange(t_len, dtype=jnp.int32),
                           jnp.arange(past + t_len, dtype=jnp.int32))
    y_c = o.reshape(bsz, t_len, SB_WIDTH)

    gates = jax.nn.sigmoid(g.reshape(bsz, t_len, N_BRANCH, D_MODEL) + b_gate)
    merged = (gates[:, :, 0] * (y_a @ w_branch[0])
              + gates[:, :, 1] * (y_b @ w_branch[1])
              + gates[:, :, 2] * (y_c @ w_branch[2]))
    x = layer_norm(DN_ALPHA * x + merged @ w_out, ln1_g, ln1_b)

    f_gate, f_val = jnp.split(x @ w_up, 2, axis=-1)
    if prompt:
        ffn_hist = jnp.zeros((bsz, FFN_CONV_WIDTH - 1, D_FF), f_gate.dtype)
    ffn_ext = jnp.concatenate([ffn_hist.astype(f_gate.dtype), f_gate], axis=1)
    f = jax.nn.gelu(causal_dwconv(ffn_ext, ffn_conv_w, ffn_conv_b), approximate=False) * f_val
    x = layer_norm(DN_ALPHA * x + f @ w_down, ln2_g, ln2_b)
    return (x, conv_ext[:, -(CONV_WIDTH - 1):], ffn_ext[:, -(FFN_CONV_WIDTH - 1):], k, vv, v_n)


def setup_inputs(seed: int = 0) -> dict:
    key = jax.random.key(seed)
    keys = iter(jax.random.split(key, 32))

    def nrm(shape, scale):
        return scale * jax.random.normal(next(keys), shape, jnp.float32)

    return {
        'x_prompt': nrm((BATCH, SEQ, D_MODEL), 1.0),
        'x_sample': nrm((DEC_BATCH, DEC_SEQ, D_MODEL), 1.0),
        'cache_sb_k': nrm((DEPTH, DEC_BATCH, PAST_LEN, SB_HEADS, SB_HEAD_DIM), 1.0),
        'cache_sb_v': nrm((DEPTH, DEC_BATCH, PAST_LEN, SB_HEADS, SB_HEAD_DIM), 1.0),
        'state_conv_glu': nrm((DEPTH, DEC_BATCH, CONV_WIDTH - 1, CONV_CH), 0.5),
        'state_ffn_conv': nrm((DEPTH, DEC_BATCH, FFN_CONV_WIDTH - 1, D_FF), 1.0),
        'w_in': nrm((DEPTH, D_MODEL, D_IN), D_MODEL ** -0.5),
        'b_gate': nrm((DEPTH, N_BRANCH, D_MODEL), 0.01),
        'conv_w': nrm((DEPTH, CONV_WIDTH, CONV_CH), CONV_WIDTH ** -0.5),
        'conv_b': nrm((DEPTH, CONV_CH), 0.01),
        'conv_ln_g': 1.0 + nrm((DEPTH, CONV_CH), 0.01),
        'conv_ln_b': nrm((DEPTH, CONV_CH), 0.01),
        'sgu_ln_g': 1.0 + nrm((DEPTH, SGU_CH), 0.01),
        'sgu_ln_b': nrm((DEPTH, SGU_CH), 0.01),
        'sgu_w': nrm((DEPTH, SGU_GROUPS, SGU_CHUNK, SGU_CHUNK), SGU_CHUNK ** -0.5),
        'sgu_b': 1.0 + nrm((DEPTH, SGU_GROUPS, SGU_CHUNK), 0.01),
        'w_branch': nrm((DEPTH, N_BRANCH, BRANCH_W, D_MODEL), DN_BETA * BRANCH_W ** -0.5),
        'w_out': nrm((DEPTH, D_MODEL, D_MODEL), DN_BETA * D_MODEL ** -0.5),
        'ln1_g': 1.0 + nrm((DEPTH, D_MODEL), 0.01),
        'ln1_b': nrm((DEPTH, D_MODEL), 0.01),
        'w_up': nrm((DEPTH, D_MODEL, 2 * D_FF), D_MODEL ** -0.5),
        'ffn_conv_w': nrm((DEPTH, FFN_CONV_WIDTH, D_FF), FFN_CONV_WIDTH ** -0.5),
        'ffn_conv_b': nrm((DEPTH, D_FF), 0.01),
        'w_down': nrm((DEPTH, D_FF, D_MODEL), DN_BETA * D_FF ** -0.5),
        'ln2_g': 1.0 + nrm((DEPTH, D_MODEL), 0.01),
        'ln2_b': nrm((DEPTH, D_MODEL), 0.01),
    }


def reference(x_prompt, x_sample, cache_sb_k, cache_sb_v, state_conv_glu, state_ffn_conv,
              w_in, b_gate, conv_w, conv_b, conv_ln_g, conv_ln_b,
              sgu_ln_g, sgu_ln_b, sgu_w, sgu_b, w_branch, w_out, ln1_g, ln1_b,
              w_up, ffn_conv_w, ffn_conv_b, w_down, ln2_g, ln2_b):
    stacked = (w_in, b_gate, conv_w, conv_b, conv_ln_g, conv_ln_b,
               sgu_ln_g, sgu_ln_b, sgu_w, sgu_b, w_branch, w_out, ln1_g, ln1_b,
               w_up, ffn_conv_w, ffn_conv_b, w_down, ln2_g, ln2_b)
    x_p, x_s = x_prompt, x_sample
    p_k, p_v, p_conv, p_ffn = [], [], [], []
    s_k, s_v, s_conv, s_ffn, s_sgu = [], [], [], [], []
    for layer in range(DEPTH):
        lw = [w[layer] for w in stacked]
        x_p, c_new, f_new, k_new, v_new, _ = trunk_layer(x_p, None, None, None, None, *lw)
        p_k.append(k_new); p_v.append(v_new); p_conv.append(c_new); p_ffn.append(f_new)
        x_s, c_new, f_new, k_new, v_new, sgu_new = trunk_layer(
            x_s, state_conv_glu[layer], state_ffn_conv[layer],
            cache_sb_k[layer], cache_sb_v[layer], *lw)
        s_k.append(k_new); s_v.append(v_new); s_conv.append(c_new); s_ffn.append(f_new)
        s_sgu.append(sgu_new)
    new_sb_k_prompt = jnp.stack(p_k)
    new_sb_v_prompt = jnp.stack(p_v)
    new_conv_glu_prompt = jnp.stack(p_conv)
    new_ffn_conv_prompt = jnp.stack(p_ffn)
    new_sb_k_sample = jnp.stack(s_k)
    new_sb_v_sample = jnp.stack(s_v)
    new_conv_glu_sample = jnp.stack(s_conv)
    new_ffn_conv_sample = jnp.stack(s_ffn)
    new_sgu_v_sample = jnp.stack(s_sgu)
    return (x_p, x_s, new_sb_k_prompt, new_sb_v_prompt, new_conv_glu_prompt, new_ffn_conv_prompt,
            new_sb_k_sample, new_sb_v_sample, new_conv_glu_sample, new_ffn_conv_sample, new_sgu_v_sample)
```

```python
import functools

import jax
import jax.numpy as jnp
from jax import lax
from jax.experimental import pallas as pl
from jax.experimental.pallas import tpu as pltpu

D_MODEL = 1024
DEPTH = 4
BRANCH_W = 512
N_BRANCH = 3
CONV_WIDTH = 31
SGU_GROUPS = 4
SGU_GROUP_CH = BRANCH_W // SGU_GROUPS
SGU_CHUNK = 128
SB_HEADS = 8
SB_HEAD_DIM = BRANCH_W // SB_HEADS
D_FF = 2816
FFN_CONV_WIDTH = 3
DN_ALPHA = (2 * DEPTH) ** 0.25
LN_EPS = 1e-5
OFF_C = 2 * BRANCH_W + 2 * BRANCH_W + 3 * BRANCH_W

MXU_DTYPE = jnp.bfloat16
LANES = 128
SUBLANES = 8
KEY_BLOCK = 128
HEADS_PER_STEP = LANES // SB_HEAD_DIM
CONV_HIST_ROWS = 32
FFN_HIST_ROWS = SUBLANES
FFN_CHUNK = 256
N_FFN_CHUNKS = D_FF // FFN_CHUNK
VMEM_LIMIT_BYTES = 56 * 1024 * 1024
assert D_FF % FFN_CHUNK == 0 and CONV_WIDTH - 1 <= CONV_HIST_ROWS and FFN_CONV_WIDTH - 1 <= FFN_HIST_ROWS
assert HEADS_PER_STEP == 2 and KEY_BLOCK & (KEY_BLOCK - 1) == 0


def _const_spec(shape):
    nd = len(shape)
    return pl.BlockSpec(shape, lambda *_: (0,) * nd, pipeline_mode=pl.Buffered(1))


def _dot(a, b):
    return jnp.dot(a.astype(MXU_DTYPE), b.astype(MXU_DTYPE), preferred_element_type=jnp.float32)


def _layer_norm(x, g, b):
    mu = jnp.mean(x, axis=-1, keepdims=True)
    xc = x - mu
    var = jnp.mean(xc * xc, axis=-1, keepdims=True)
    return xc * lax.rsqrt(var + LN_EPS) * g + b


def _gelu(x):
    return 0.5 * x * (1.0 + lax.erf(x * (2.0 ** -0.5)))


def _mixer_in_kernel(x_ref, hist_ref, w_ref, cw_ref, cb_ref, clg_ref, clb_ref,
                     slg_ref, slb_ref, sw_ref, sbt_ref,
                     ya_ref, yb_ref, q_ref, k_ref, v_ref, kb_ref, vb_ref, cst_ref, *rest,
                     tm, chunk, emit_vn):
    if emit_vn:
        vn_ref, ext_ref = rest
    else:
        (ext_ref,) = rest
    i = pl.program_id(1)
    xb = x_ref[0].astype(MXU_DTYPE)

    a = jnp.dot(xb, w_ref[:, 0:2 * BRANCH_W], preferred_element_type=jnp.float32)
    glu = a[:, :BRANCH_W] * jax.nn.sigmoid(a[:, BRANCH_W:])

    @pl.when(i == 0)
    def _():
        ext_ref[0:CONV_HIST_ROWS, :] = hist_ref[0]

    ext_ref[pl.ds(CONV_HIST_ROWS, tm), :] = glu
    first = CONV_HIST_ROWS - (CONV_WIDTH - 1)
    rows = min(tm, 64)
    for r in range(0, tm, rows):
        acc = jnp.zeros((rows, BRANCH_W), jnp.float32) + cb_ref[...]
        for j in range(CONV_WIDTH):
            acc = acc + ext_ref[pl.ds(r + first + j, rows), :] * cw_ref[j:j + 1, :]
        y = _layer_norm(acc, clg_ref[...], clb_ref[...])
        ya_ref[0, pl.ds(r, rows), :] = (y * jax.nn.sigmoid(y)).astype(ya_ref.dtype)
    cst_ref[0] = ext_ref[pl.ds(tm + first, CONV_WIDTH - 1), :]
    ext_ref[0:CONV_HIST_ROWS, :] = ext_ref[pl.ds(tm, CONV_HIST_ROWS), :]

    uv = jnp.dot(xb, w_ref[:, 2 * BRANCH_W:4 * BRANCH_W], preferred_element_type=jnp.float32)
    gu = _gelu(uv[:, :BRANCH_W])
    vn = _layer_norm(_gelu(uv[:, BRANCH_W:]), slg_ref[...], slb_ref[...])
    if emit_vn:
        vn_ref[0] = vn
    vnb = vn.astype(MXU_DTYPE)
    tril = (lax.broadcasted_iota(jnp.int32, (chunk, chunk), 1)
            <= lax.broadcasted_iota(jnp.int32, (chunk, chunk), 0))
    for g in range(SGU_GROUPS):
        wg = jnp.where(tril, sw_ref[g], 0.0).astype(MXU_DTYPE)
        bias = sbt_ref[:, g:g + 1]
        cols = slice(g * SGU_GROUP_CH, (g + 1) * SGU_GROUP_CH)
        for c in range(tm // chunk):
            rws = slice(c * chunk, (c + 1) * chunk)
            mixed = jnp.dot(wg, vnb[rws, cols], preferred_element_type=jnp.float32) + bias
            yb_ref[0, rws, cols] = (gu[rws, cols] * mixed).astype(yb_ref.dtype)

    qkv = jnp.dot(xb, w_ref[:, 4 * BRANCH_W:7 * BRANCH_W], preferred_element_type=jnp.float32)
    q_ref[0] = (qkv[:, :BRANCH_W] * (SB_HEAD_DIM ** -0.5)).astype(q_ref.dtype)
    k = qkv[:, BRANCH_W:2 * BRANCH_W]
    v = qkv[:, 2 * BRANCH_W:]
    k_ref[0] = k
    v_ref[0] = v
    kb_ref[0] = k.astype(kb_ref.dtype)
    vb_ref[0] = v.astype(vb_ref.dtype)


def _mixer_in(x, conv_hist, w_main, conv_w, conv_b, cln_g, cln_b, sln_g, sln_b, sgu_w, sgu_bt,
              *, tm, chunk, emit_vn):
    bsz, t_len, _ = x.shape
    assert t_len % tm == 0 and tm % chunk == 0 and tm >= CONV_HIST_ROWS
    grid = (bsz, t_len // tm)
    row_spec = lambda w: pl.BlockSpec((1, tm, w), lambda b, i: (b, i, 0))
    f32 = jnp.float32
    out_shape = [
        jax.ShapeDtypeStruct((bsz, t_len, BRANCH_W), MXU_DTYPE),
        jax.ShapeDtypeStruct((bsz, t_len, BRANCH_W), MXU_DTYPE),
        jax.ShapeDtypeStruct((bsz, t_len, BRANCH_W), MXU_DTYPE),
        jax.ShapeDtypeStruct((bsz, t_len, BRANCH_W), f32),
        jax.ShapeDtypeStruct((bsz, t_len, BRANCH_W), f32),
        jax.ShapeDtypeStruct((bsz, t_len, BRANCH_W), MXU_DTYPE),
        jax.ShapeDtypeStruct((bsz, t_len, BRANCH_W), MXU_DTYPE),
        jax.ShapeDtypeStruct((bsz, CONV_WIDTH - 1, BRANCH_W), f32),
    ]
    out_specs = [row_spec(BRANCH_W)] * 7 + [
        pl.BlockSpec((1, CONV_WIDTH - 1, BRANCH_W), lambda b, i: (b, 0, 0))]
    if emit_vn:
        out_shape.append(jax.ShapeDtypeStruct((bsz, t_len, BRANCH_W), f32))
        out_specs.append(row_spec(BRANCH_W))
    in_specs = [
        row_spec(D_MODEL),
        pl.BlockSpec((1, CONV_HIST_ROWS, BRANCH_W), lambda b, i: (b, 0, 0)),
        _const_spec(w_main.shape), _const_spec(conv_w.shape), _const_spec(conv_b.shape),
        _const_spec(cln_g.shape), _const_spec(cln_b.shape),
        _const_spec(sln_g.shape), _const_spec(sln_b.shape),
        _const_spec(sgu_w.shape), _const_spec(sgu_bt.shape),
    ]
    return pl.pallas_call(
        functools.partial(_mixer_in_kernel, tm=tm, chunk=chunk, emit_vn=emit_vn),
        grid=grid, in_specs=in_specs, out_specs=out_specs, out_shape=out_shape,
        scratch_shapes=[pltpu.VMEM((CONV_HIST_ROWS + tm, BRANCH_W), f32)],
        compiler_params=pltpu.CompilerParams(
            dimension_semantics=("arbitrary", "arbitrary"), vmem_limit_bytes=VMEM_LIMIT_BYTES),
        name="mixer_in",
    )(x, conv_hist, w_main, conv_w, conv_b, cln_g, cln_b, sln_g, sln_b, sgu_w, sgu_bt)


def _softplus(z):
    return jnp.maximum(z, 0.0) + jnp.log(1.0 + jnp.exp(-jnp.abs(z)))


def _attention_kernel(q_ref, kn_ref, vn_ref, *rest, tq, n_hist_blocks):
    if n_hist_blocks:
        kh_ref, vh_ref, o_ref, acc_ref, run_ref = rest
    else:
        o_ref, acc_ref, run_ref = rest
    i = pl.program_id(2)
    two = HEADS_PER_STEP * KEY_BLOCK
    lane = lax.broadcasted_iota(jnp.int32, (KEY_BLOCK, LANES), 1)
    head0 = lane < SB_HEAD_DIM
    r_i = lax.broadcasted_iota(jnp.int32, (two, two), 0)
    c_i = lax.broadcasted_iota(jnp.int32, (two, two), 1)
    cum = ((r_i >= c_i) & ((r_i < KEY_BLOCK) == (c_i < KEY_BLOCK))).astype(MXU_DTYPE)

    acc_ref[...] = jnp.zeros_like(acc_ref)
    run_ref[...] = jnp.zeros_like(run_ref)

    def split_heads(blk):
        zero = jnp.zeros_like(blk)
        return jnp.concatenate([jnp.where(head0, blk, zero), jnp.where(head0, zero, blk)], axis=0)

    def step(r0, kblk, vblk, diagonal):
        m = tq - r0
        q = q_ref[0, pl.ds(r0, m), :]
        z = lax.dot_general(q, split_heads(kblk), (((1,), (1,)), ((), ())),
                            preferred_element_type=jnp.float32)
        sp = _softplus(z)
        if diagonal:
            t_loc = lax.broadcasted_iota(jnp.int32, (m, two), 0)
            s_loc = lax.broadcasted_iota(jnp.int32, (m, two), 1) & (KEY_BLOCK - 1)
            valid = s_loc < t_loc
            sp = jnp.where(valid, sp, 0.0)
        hi = sp.astype(MXU_DTYPE)
        lo = (sp - hi.astype(jnp.float32)).astype(MXU_DTYPE)
        csum = (jnp.dot(hi, cum, preferred_element_type=jnp.float32)
                + jnp.dot(lo, cum, preferred_element_type=jnp.float32))
        run = run_ref[pl.ds(r0, m), :]
        w = jnp.exp(z - csum - run)
        if diagonal:
            w = jnp.where(valid, w, 0.0)
        acc_ref[pl.ds(r0, m), :] += jnp.dot(w.astype(MXU_DTYPE), split_heads(vblk),
                                            preferred_element_type=jnp.float32)
        tot = jnp.concatenate(
            [jnp.broadcast_to(csum[:, h * KEY_BLOCK:h * KEY_BLOCK + 1], (m, KEY_BLOCK))
             for h in range(HEADS_PER_STEP)], axis=1)
        run_ref[pl.ds(r0, m), :] = run + tot

    diag_rows = min(tq, KEY_BLOCK)
    for jj in reversed(range(tq // diag_rows)):
        start = pl.multiple_of(i * tq + jj * diag_rows, diag_rows)
        step(jj * diag_rows, kn_ref[0, pl.ds(start, KEY_BLOCK), :],
             vn_ref[0, pl.ds(start, KEY_BLOCK), :], True)

    n_prev = i * (tq // KEY_BLOCK)

    def prev_body(it, carry):
        start = pl.multiple_of((n_prev - 1 - it) * KEY_BLOCK, KEY_BLOCK)
        step(0, kn_ref[0, pl.ds(start, KEY_BLOCK), :], vn_ref[0, pl.ds(start, KEY_BLOCK), :], False)
        return carry

    lax.fori_loop(0, n_prev, prev_body, 0)

    if n_hist_blocks:
        def hist_body(it, carry):
            start = pl.multiple_of((n_hist_blocks - 1 - it) * KEY_BLOCK, KEY_BLOCK)
            step(0, kh_ref[0, pl.ds(start, KEY_BLOCK), :], vh_ref[0, pl.ds(start, KEY_BLOCK), :], False)
            return carry

        lax.fori_loop(0, n_hist_blocks, hist_body, 0)

    o_ref[0] = acc_ref[...].astype(o_ref.dtype)


def _attention(q, k_new, v_new, k_hist, v_hist, *, tq):
    bsz, t_len, _ = q.shape
    t_keys = k_new.shape[1]
    assert t_len % tq == 0 and (tq % KEY_BLOCK == 0 or tq == t_len) and t_keys % KEY_BLOCK == 0
    n_pairs = BRANCH_W // LANES
    grid = (bsz, n_pairs, t_len // tq)
    q_spec = pl.BlockSpec((1, tq, LANES), lambda b, p, i: (b, i, p))
    new_spec = pl.BlockSpec((1, t_keys, LANES), lambda b, p, i: (b, 0, p))
    in_specs = [q_spec, new_spec, new_spec]
    args = [q, k_new, v_new]
    n_hist_blocks = 0
    if k_hist is not None:
        past = k_hist.shape[1]
        assert past % KEY_BLOCK == 0
        n_hist_blocks = past // KEY_BLOCK
        hist_spec = pl.BlockSpec((1, past, LANES), lambda b, p, i: (b, 0, p))
        in_specs += [hist_spec, hist_spec]
        args += [k_hist, v_hist]
    return pl.pallas_call(
        functools.partial(_attention_kernel, tq=tq, n_hist_blocks=n_hist_blocks),
        grid=grid, in_specs=in_specs, out_specs=q_spec,
        out_shape=jax.ShapeDtypeStruct((bsz, t_len, BRANCH_W), MXU_DTYPE),
        scratch_shapes=[pltpu.VMEM((tq, LANES), jnp.float32),
                        pltpu.VMEM((tq, HEADS_PER_STEP * KEY_BLOCK), jnp.float32)],
        compiler_params=pltpu.CompilerParams(
            dimension_semantics=("arbitrary", "arbitrary", "arbitrary"),
            vmem_limit_bytes=VMEM_LIMIT_BYTES),
        name="stick_breaking",
    )(*args)


def _merge_kernel(x_ref, ya_ref, yb_ref, yc_ref, wg_ref, bg_ref, wb_ref, wo_ref, g_ref, b_ref, o_ref):
    x = x_ref[...]
    xb = x.astype(MXU_DTYPE)
    merged = None
    for br, y_ref in enumerate((ya_ref, yb_ref, yc_ref)):
        logits = jnp.dot(xb, wg_ref[:, br * D_MODEL:(br + 1) * D_MODEL],
                         preferred_element_type=jnp.float32) + bg_ref[br:br + 1, :]
        proj = jnp.dot(y_ref[...], wb_ref[br], preferred_element_type=jnp.float32)
        term = jax.nn.sigmoid(logits) * proj
        merged = term if merged is None else merged + term
    h = DN_ALPHA * x + _dot(merged, wo_ref[...])
    o_ref[...] = _layer_norm(h, g_ref[...], b_ref[...])


def _merge(x, y_a, y_b, y_c, w_gate, b_gate, w_branch, w_out, ln_g, ln_b, *, tm):
    n = x.shape[0]
    assert n % tm == 0
    row_spec = lambda w: pl.BlockSpec((tm, w), lambda i: (i, 0))
    return pl.pallas_call(
        _merge_kernel, grid=(n // tm,),
        in_specs=[row_spec(D_MODEL), row_spec(BRANCH_W), row_spec(BRANCH_W), row_spec(BRANCH_W),
                  _const_spec(w_gate.shape), _const_spec(b_gate.shape), _const_spec(w_branch.shape),
                  _const_spec(w_out.shape), _const_spec(ln_g.shape), _const_spec(ln_b.shape)],
        out_specs=row_spec(D_MODEL),
        out_shape=jax.ShapeDtypeStruct((n, D_MODEL), jnp.float32),
        compiler_params=pltpu.CompilerParams(
            dimension_semantics=("arbitrary",), vmem_limit_bytes=VMEM_LIMIT_BYTES),
        name="merge",
    )(x, y_a, y_b, y_c, w_gate, b_gate, w_branch, w_out, ln_g, ln_b)


def _ffn_kernel(x_ref, hist_ref, wug_ref, wuv_ref, cw_ref, cb_ref, wd_ref, g_ref, b_ref,
                o_ref, st_ref, ext_ref, *, tm):
    i = pl.program_id(1)
    x = x_ref[0]
    xb = x.astype(MXU_DTYPE)

    @pl.when(i == 0)
    def _():
        ext_ref[:, 0:FFN_HIST_ROWS, :] = hist_ref[0]

    acc = jnp.zeros((tm, D_MODEL), jnp.float32)
    for c in range(N_FFN_CHUNKS):
        f_gate = jnp.dot(xb, wug_ref[c], preferred_element_type=jnp.float32)
        f_val = jnp.dot(xb, wuv_ref[c], preferred_element_type=jnp.float32)
        ext_ref[c, pl.ds(FFN_HIST_ROWS, tm), :] = f_gate
        conv = cb_ref[c] + f_gate * cw_ref[c, FFN_CONV_WIDTH - 1:FFN_CONV_WIDTH, :]
        for j in range(FFN_CONV_WIDTH - 1):
            off = FFN_HIST_ROWS - (FFN_CONV_WIDTH - 1) + j
            conv = conv + ext_ref[c, pl.ds(off, tm), :] * cw_ref[c, j:j + 1, :]
        f = _gelu(conv) * f_val
        acc = acc + jnp.dot(f.astype(MXU_DTYPE), wd_ref[c], preferred_element_type=jnp.float32)
        st_ref[0, c] = ext_ref[c, pl.ds(tm + FFN_HIST_ROWS - (FFN_CONV_WIDTH - 1), FFN_CONV_WIDTH - 1), :]
        ext_ref[c, 0:FFN_HIST_ROWS, :] = ext_ref[c, pl.ds(tm, FFN_HIST_ROWS), :]
    o_ref[0] = _layer_norm(DN_ALPHA * x + acc, g_ref[...], b_ref[...])


def _ffn(x, ffn_hist, w_up_gate, w_up_val, conv_w, conv_b, w_down, ln_g, ln_b, *, tm):
    bsz, t_len, _ = x.shape
    assert t_len % tm == 0 and tm >= FFN_HIST_ROWS
    row_spec = pl.BlockSpec((1, tm, D_MODEL), lambda b, i: (b, i, 0))
    return pl.pallas_call(
        functools.partial(_ffn_kernel, tm=tm), grid=(bsz, t_len // tm),
        in_specs=[row_spec,
                  pl.BlockSpec((1, N_FFN_CHUNKS, FFN_HIST_ROWS, FFN_CHUNK), lambda b, i: (b, 0, 0, 0)),
                  _const_spec(w_up_gate.shape), _const_spec(w_up_val.shape), _const_spec(conv_w.shape),
                  _const_spec(conv_b.shape), _const_spec(w_down.shape),
                  _const_spec(ln_g.shape), _const_spec(ln_b.shape)],
        out_specs=[row_spec,
                   pl.BlockSpec((1, N_FFN_CHUNKS, FFN_CONV_WIDTH - 1, FFN_CHUNK), lambda b, i: (b, 0, 0, 0))],
        out_shape=[jax.ShapeDtypeStruct((bsz, t_len, D_MODEL), jnp.float32),
                   jax.ShapeDtypeStruct((bsz, N_FFN_CHUNKS, FFN_CONV_WIDTH - 1, FFN_CHUNK), jnp.float32)],
        scratch_shapes=[pltpu.VMEM((N_FFN_CHUNKS, FFN_HIST_ROWS + tm, FFN_CHUNK), jnp.float32)],
        compiler_params=pltpu.CompilerParams(
            dimension_semantics=("arbitrary", "arbitrary"), vmem_limit_bytes=VMEM_LIMIT_BYTES),
        name="ffn",
    )(x, ffn_hist, w_up_gate, w_up_val, conv_w, conv_b, w_down, ln_g, ln_b)


def _chunk_cols(a):
    parts = a.reshape(a.shape[:-1] + (N_FFN_CHUNKS, FFN_CHUNK))
    return jnp.moveaxis(parts, -2, 0)


def _prepare_layer_weights(w_in, b_gate, conv_w, conv_b, conv_ln_g, conv_ln_b, sgu_ln_g, sgu_ln_b,
                           sgu_w, sgu_b, w_branch, w_out, ln1_g, ln1_b, w_up, ffn_conv_w, ffn_conv_b,
                           w_down, ln2_g, ln2_b):
    row = lambda a: a.reshape(1, -1)
    return dict(
        w_main=w_in[:, :OFF_C].astype(MXU_DTYPE), w_gate=w_in[:, OFF_C:].astype(MXU_DTYPE),
        b_gate=b_gate, conv_w=conv_w, conv_b=row(conv_b), cln_g=row(conv_ln_g), cln_b=row(conv_ln_b),
        sln_g=row(sgu_ln_g), sln_b=row(sgu_ln_b), sgu_w=sgu_w, sgu_bt=sgu_b.T,
        w_branch=w_branch.astype(MXU_DTYPE), w_out=w_out.astype(MXU_DTYPE),
        ln1_g=row(ln1_g), ln1_b=row(ln1_b),
        w_up_gate=_chunk_cols(w_up[:, :D_FF]).astype(MXU_DTYPE),
        w_up_val=_chunk_cols(w_up[:, D_FF:]).astype(MXU_DTYPE),
        ffn_conv_w=_chunk_cols(ffn_conv_w), ffn_conv_b=_chunk_cols(row(ffn_conv_b)),
        w_down=w_down.reshape(N_FFN_CHUNKS, FFN_CHUNK, D_MODEL).astype(MXU_DTYPE),
        ln2_g=row(ln2_g), ln2_b=row(ln2_b),
    )


def _trunk_layer(x, conv_hist, ffn_hist, k_hist, v_hist, lw, *, tm, tq, chunk, emit_vn):
    bsz, t_len, _ = x.shape
    conv_hist = jnp.pad(conv_hist, ((0, 0), (CONV_HIST_ROWS - (CONV_WIDTH - 1), 0), (0, 0)))
    ffn_hist = jnp.pad(_chunk_cols(ffn_hist).transpose(1, 0, 2, 3),
                       ((0, 0), (0, 0), (FFN_HIST_ROWS - (FFN_CONV_WIDTH - 1), 0), (0, 0)))
    outs = _mixer_in(x, conv_hist, lw["w_main"], lw["conv_w"], lw["conv_b"], lw["cln_g"], lw["cln_b"],
                     lw["sln_g"], lw["sln_b"], lw["sgu_w"][:, :chunk, :chunk], lw["sgu_bt"][:chunk],
                     tm=tm, chunk=chunk, emit_vn=emit_vn)
    y_a, y_b, q, k, v, k_op, v_op, conv_state = outs[:8]
    v_n = outs[8] if emit_vn else None
    pad_keys = -t_len % KEY_BLOCK
    if pad_keys:
        k_op = jnp.pad(k_op, ((0, 0), (0, pad_keys), (0, 0)))
        v_op = jnp.pad(v_op, ((0, 0), (0, pad_keys), (0, 0)))
    y_c = _attention(q, k_op, v_op, k_hist, v_hist, tq=tq)
    flat = lambda a: a.reshape(bsz * t_len, a.shape[-1])
    x1 = _merge(flat(x), flat(y_a), flat(y_b), flat(y_c), lw["w_gate"], lw["b_gate"], lw["w_branch"],
                lw["w_out"], lw["ln1_g"], lw["ln1_b"], tm=min(bsz * t_len, 512))
    x2, ffn_state = _ffn(x1.reshape(bsz, t_len, D_MODEL), ffn_hist, lw["w_up_gate"], lw["w_up_val"],
                         lw["ffn_conv_w"], lw["ffn_conv_b"], lw["w_down"], lw["ln2_g"], lw["ln2_b"], tm=tm)
    ffn_state = ffn_state.transpose(0, 2, 1, 3).reshape(bsz, FFN_CONV_WIDTH - 1, D_FF)
    return x2, conv_state, ffn_state, k, v, v_n


def kernel(x_prompt, x_sample, cache_sb_k, cache_sb_v, state_conv_glu, state_ffn_conv, w_in, b_gate, conv_w, conv_b, conv_ln_g, conv_ln_b, sgu_ln_g, sgu_ln_b, sgu_w, sgu_b, w_branch, w_out, ln1_g, ln1_b, w_up, ffn_conv_w, ffn_conv_b, w_down, ln2_g, ln2_b):
    stacked = (w_in, b_gate, conv_w, conv_b, conv_ln_g, conv_ln_b, sgu_ln_g, sgu_ln_b, sgu_w, sgu_b,
               w_branch, w_out, ln1_g, ln1_b, w_up, ffn_conv_w, ffn_conv_b, w_down, ln2_g, ln2_b)
    bsz, seq, _ = x_prompt.shape
    dec_bsz, dec_seq, _ = x_sample.shape
    past = cache_sb_k.shape[2]
    heads = lambda a: a.reshape(a.shape[0], a.shape[1], SB_HEADS, SB_HEAD_DIM)
    x_p, x_s = x_prompt, x_sample
    p_k, p_v, p_conv, p_ffn = [], [], [], []
    s_k, s_v, s_conv, s_ffn, s_sgu = [], [], [], [], []
    for layer in range(DEPTH):
        lw = _prepare_layer_weights(*[w[layer] for w in stacked])
        x_p, c_new, f_new, k_new, v_new, _ = _trunk_layer(
            x_p, jnp.zeros((bsz, CONV_WIDTH - 1, BRANCH_W), jnp.float32),
            jnp.zeros((bsz, FFN_CONV_WIDTH - 1, D_FF), jnp.float32), None, None, lw,
            tm=512, tq=256, chunk=SGU_CHUNK, emit_vn=False)
        p_k.append(heads(k_new)); p_v.append(heads(v_new)); p_conv.append(c_new); p_ffn.append(f_new)
        k_hist = cache_sb_k[layer].reshape(dec_bsz, past, BRANCH_W).astype(MXU_DTYPE)
        v_hist = cache_sb_v[layer].reshape(dec_bsz, past, BRANCH_W).astype(MXU_DTYPE)
        x_s, c_new, f_new, k_new, v_new, sgu_new = _trunk_layer(
            x_s, state_conv_glu[layer], state_ffn_conv[layer], k_hist, v_hist, lw,
            tm=dec_seq, tq=dec_seq, chunk=dec_seq, emit_vn=True)
        s_k.append(heads(k_new)); s_v.append(heads(v_new)); s_conv.append(c_new); s_ffn.append(f_new)
        s_sgu.append(sgu_new)
    return (x_p, x_s, jnp.stack(p_k), jnp.stack(p_v), jnp.stack(p_conv), jnp.stack(p_ffn),
            jnp.stack(s_k), jnp.stack(s_v), jnp.stack(s_conv), jnp.stack(s_ffn), jnp.stack(s_sgu))
```
